```python
import jax, jax.numpy as jnp
from jax import lax
import numpy as np

D_MODEL = 1024
BATCH = 1
SEQ = 16384
DEPTH = 1

CHUNK = 64
MIX_WIDTH = D_MODEL
CONV_WIDTH = MIX_WIDTH // 2
POOL_WIDTH = MIX_WIDTH - CONV_WIDTH
CONV_HEADS = 8
CONV_K = 3
POOL_WINDOWS = (2, 4, 8, 16)
POOL_GROUPS = len(POOL_WINDOWS)
POOL_GROUP_WIDTH = POOL_WIDTH // POOL_GROUPS
IN_PROJ_WIDTH = 3 * CONV_WIDTH + POOL_WIDTH
D_FF = 4 * D_MODEL
EPS = 1e-6

kernel_name = "hybrid_shortconv_multipool_block"


def rmsnorm(x, g):
    xf = x.astype(jnp.float32)
    r = lax.rsqrt(jnp.mean(xf * xf, axis=-1, keepdims=True) + EPS)
    return (xf * r).astype(x.dtype) * g


def short_gated_conv(h, gate_b, gate_c, conv_w):
    s = h.shape[1]
    z = gate_c * h
    zp = jnp.pad(z, ((0, 0), (CONV_K - 1, 0), (0, 0)))
    conv = conv_w[0] * zp[:, 0:s] + conv_w[1] * zp[:, 1:s + 1] + conv_w[2] * zp[:, 2:s + 2]
    return gate_b * conv


def multiscale_causal_pool(v, pool_w, pool_scale):
    b, s, _ = v.shape
    cs = jnp.cumsum(v.astype(jnp.float32), axis=1)
    pos = jnp.arange(s, dtype=jnp.float32)
    outs = []
    for gi, w in enumerate(POOL_WINDOWS):
        lo, hi = gi * POOL_GROUP_WIDTH, (gi + 1) * POOL_GROUP_WIDTH
        cs_g = cs[..., lo:hi]
        cs_shift = jnp.pad(cs_g, ((0, 0), (w, 0), (0, 0)))[:, :s]
        count = jnp.minimum(pos + 1.0, float(w))[None, :, None]
        mean = (cs_g - cs_shift) / count
        outs.append((mean - v[..., lo:hi].astype(jnp.float32)).astype(v.dtype))
    u = jnp.stack(outs, axis=2)
    y = jnp.einsum('bsgc,gcd->bsgd', u, pool_w)
    return y.reshape(b, s, POOL_WIDTH) * pool_scale


def setup_inputs(seed: int = 0) -> dict:
    key = jax.random.key(seed)
    ks = jax.random.split(key, 12)
    f32 = jnp.float32
    x = jax.random.normal(ks[0], (BATCH, SEQ, D_MODEL), f32)
    norm_mix_g = 1.0 + 0.05 * jax.random.normal(ks[1], (DEPTH, D_MODEL), f32)
    w_in = jax.random.normal(ks[2], (DEPTH, D_MODEL, IN_PROJ_WIDTH), f32) * D_MODEL ** -0.5
    conv_w = jax.random.normal(ks[3], (DEPTH, CONV_K, CONV_WIDTH), f32) * CONV_K ** -0.5
    pool_w = jax.random.normal(ks[4], (DEPTH, POOL_GROUPS, POOL_GROUP_WIDTH, POOL_GROUP_WIDTH), f32) * POOL_GROUP_WIDTH ** -0.5
    pool_scale = 1.0 + 0.1 * jax.random.normal(ks[5], (DEPTH, POOL_WIDTH), f32)
    w_out = jax.random.normal(ks[6], (DEPTH, MIX_WIDTH, D_MODEL), f32) * MIX_WIDTH ** -0.5
    norm_mlp_g = 1.0 + 0.05 * jax.random.normal(ks[7], (DEPTH, D_MODEL), f32)
    w_ff1 = jax.random.normal(ks[8], (DEPTH, D_MODEL, D_FF), f32) * D_MODEL ** -0.5
    w_ff2 = jax.random.normal(ks[9], (DEPTH, D_FF, D_MODEL), f32) * D_FF ** -0.5
    norm_final_g = 1.0 + 0.05 * jax.random.normal(ks[10], (D_MODEL,), f32)
    return {"x": x, "norm_mix_g": norm_mix_g, "w_in": w_in, "conv_w": conv_w,
            "pool_w": pool_w, "pool_scale": pool_scale, "w_out": w_out,
            "norm_mlp_g": norm_mlp_g, "w_ff1": w_ff1, "w_ff2": w_ff2,
            "norm_final_g": norm_final_g}


def reference(x, norm_mix_g, w_in, conv_w, pool_w, pool_scale, w_out,
              norm_mlp_g, w_ff1, w_ff2, norm_final_g):
    for l in range(DEPTH):
        xn = rmsnorm(x, norm_mix_g[l])
        proj = jnp.einsum('bsd,de->bse', xn, w_in[l])
        h = proj[..., 0:CONV_WIDTH]
        gate_b = proj[..., CONV_WIDTH:2 * CONV_WIDTH]
        gate_c = proj[..., 2 * CONV_WIDTH:3 * CONV_WIDTH]
        v = proj[..., 3 * CONV_WIDTH:]
        y_conv = short_gated_conv(h, gate_b, gate_c, conv_w[l])
        y_pool = multiscale_causal_pool(v, pool_w[l], pool_scale[l])
        y = jnp.concatenate([y_conv, y_pool], axis=-1)
        x = x + jnp.einsum('bse,ed->bsd', y, w_out[l])
        xn = rmsnorm(x, norm_mlp_g[l])
        hid = jnp.square(jax.nn.relu(jnp.einsum('bsd,df->bsf', xn, w_ff1[l])))
        x = x + jnp.einsum('bsf,fd->bsd', hid, w_ff2[l])
    return rmsnorm(x, norm_final_g)
```

```python
import functools

import jax
import jax.numpy as jnp
from jax.experimental import pallas as pl
from jax.experimental.pallas import tpu as pltpu

CONV_K = 3
POOL_WINDOWS = (2, 4, 8, 16)
EPS = 1e-6

V7X_SUBLANES = 8
V7X_LANES = 128
V7X_MXU_DIM = 256
V7X_VMEM_BYTES = 64 * 1024 * 1024

ROW_TILE = 512
FF_CHUNK = 1024
CONV_HALO = V7X_SUBLANES
POOL_HALO = 2 * V7X_SUBLANES


def _rmsnorm(x, g):
    r = jax.lax.rsqrt(jnp.mean(x * x, axis=-1, keepdims=True) + EPS)
    return (x * r) * g


def _bdot(a, b):
    return jnp.dot(a, b, preferred_element_type=jnp.float32)


def _block_kernel(x_ref, g_mix_ref, w_in_ref, conv_w_ref, pool_w_ref, pool_scale_ref,
                  w_out_ref, g_mlp_ref, w_ff1_ref, w_ff2_ref, g_final_ref,
                  o_ref, z_hist, v_hist, *, conv_width, pool_group_width):
    tile = x_ref.shape[0]
    d_ff = w_ff1_ref.shape[1]
    step = pl.program_id(0)
    bf16 = jnp.bfloat16

    @pl.when(step == 0)
    def _():
        z_hist[0:CONV_HALO, :] = jnp.zeros((CONV_HALO, z_hist.shape[1]), jnp.float32)
        v_hist[0:POOL_HALO, :] = jnp.zeros((POOL_HALO, v_hist.shape[1]), jnp.float32)

    x = x_ref[...]
    xn = _rmsnorm(x, g_mix_ref[...])
    proj = _bdot(xn.astype(bf16), w_in_ref[...])
    c = conv_width
    h, gate_b, gate_c, v = proj[:, 0:c], proj[:, c:2 * c], proj[:, 2 * c:3 * c], proj[:, 3 * c:]

    z = gate_c * h
    z_hist[CONV_HALO:CONV_HALO + tile, :] = z
    conv_w = conv_w_ref[...]
    conv = conv_w[CONV_K - 1:CONV_K, :] * z
    for k in range(CONV_K - 1):
        back = CONV_K - 1 - k
        conv = conv + conv_w[k:k + 1, :] * z_hist[CONV_HALO - back:CONV_HALO - back + tile, :]
    y_conv = gate_b * conv
    z_hist[0:CONV_HALO, :] = z_hist[tile:tile + CONV_HALO, :]

    v_hist[POOL_HALO:POOL_HALO + tile, :] = v
    pos = step * tile + jax.lax.broadcasted_iota(jnp.int32, (tile, 1), 0)
    frames_seen = (pos + 1).astype(jnp.float32)
    gw = pool_group_width
    us = []
    for gi, w in enumerate(POOL_WINDOWS):
        lo = gi * gw
        vg = v[:, lo:lo + gw]
        s = vg
        for back in range(1, w):
            s = s + v_hist[POOL_HALO - back:POOL_HALO - back + tile, lo:lo + gw]
        us.append(s / jnp.minimum(frames_seen, float(w)) - vg)
    v_hist[0:POOL_HALO, :] = v_hist[tile:tile + POOL_HALO, :]
    u = jnp.concatenate(us, axis=-1).astype(bf16)
    mx = V7X_MXU_DIM
    y_pool = jnp.concatenate(
        [_bdot(u[:, j * mx:(j + 1) * mx], pool_w_ref[j]) for j in range(pool_w_ref.shape[0])],
        axis=-1) * pool_scale_ref[...]

    y = jnp.concatenate([y_conv, y_pool], axis=-1).astype(bf16)
    x = x + _bdot(y, w_out_ref[...])

    xn = _rmsnorm(x, g_mlp_ref[...]).astype(bf16)
    for f0 in range(0, d_ff, FF_CHUNK):
        hid = jnp.square(jnp.maximum(_bdot(xn, w_ff1_ref[:, f0:f0 + FF_CHUNK]), 0.0))
        x = x + _bdot(hid.astype(bf16), w_ff2_ref[f0:f0 + FF_CHUNK, :])

    o_ref[...] = _rmsnorm(x, g_final_ref[...])


def _block_diag_tiles(pool_w, tile_dim):
    g, cg, _ = pool_w.shape
    per = tile_dim // cg
    tiles = []
    for j in range(g // per):
        t = jnp.zeros((tile_dim, tile_dim), pool_w.dtype)
        for k in range(per):
            t = t.at[k * cg:(k + 1) * cg, k * cg:(k + 1) * cg].set(pool_w[j * per + k])
        tiles.append(t)
    return jnp.stack(tiles)


def _resident(shape):
    return pl.BlockSpec(shape, lambda i: (0,) * len(shape), pipeline_mode=pl.Buffered(1))


def _layer(x, g_mix, w_in, conv_w, pool_w, pool_scale, w_out, g_mlp, w_ff1, w_ff2, g_final):
    seq, d_model = x.shape
    conv_width = conv_w.shape[1]
    pool_groups, pool_group_width, _ = pool_w.shape
    pool_width = pool_groups * pool_group_width
    assert seq % ROW_TILE == 0 and w_ff1.shape[1] % FF_CHUNK == 0
    assert pool_groups == len(POOL_WINDOWS) and max(POOL_WINDOWS) - 1 <= POOL_HALO
    assert conv_w.shape[0] == CONV_K and CONV_K - 1 <= CONV_HALO
    assert w_in.shape[1] == 3 * conv_width + pool_width
    assert V7X_MXU_DIM % pool_group_width == 0 and pool_width % V7X_MXU_DIM == 0

    bf16 = jnp.bfloat16
    operands = (
        x,
        g_mix.reshape(1, d_model),
        w_in.astype(bf16),
        conv_w,
        _block_diag_tiles(pool_w.astype(bf16), V7X_MXU_DIM),
        pool_scale.reshape(1, pool_width),
        w_out.astype(bf16),
        g_mlp.reshape(1, d_model),
        w_ff1.astype(bf16),
        w_ff2.astype(bf16),
        g_final.reshape(1, d_model),
    )
    row_spec = pl.BlockSpec((ROW_TILE, d_model), lambda i: (i, 0))
    in_specs = [row_spec] + [_resident(a.shape) for a in operands[1:]]
    body = functools.partial(_block_kernel, conv_width=conv_width,
                             pool_group_width=pool_group_width)
    return pl.pallas_call(
        body,
        grid=(seq // ROW_TILE,),
        in_specs=in_specs,
        out_specs=row_spec,
        out_shape=jax.ShapeDtypeStruct((seq, d_model), x.dtype),
        scratch_shapes=[
            pltpu.VMEM((CONV_HALO + ROW_TILE, conv_width), jnp.float32),
            pltpu.VMEM((POOL_HALO + ROW_TILE, pool_width), jnp.float32),
        ],
        compiler_params=pltpu.CompilerParams(
            dimension_semantics=("arbitrary",),
            vmem_limit_bytes=V7X_VMEM_BYTES * 7 // 8,
        ),
        name="hybrid_block",
    )(*operands)


def kernel(x, norm_mix_g, w_in, conv_w, pool_w, pool_scale, w_out, norm_mlp_g, w_ff1, w_ff2, norm_final_g):
    batch, seq, d_model = x.shape
    depth = w_in.shape[0]
    assert depth == 1, "the final norm is fused into the (single) layer call"
    assert batch == 1
    out = _layer(x.reshape(seq, d_model), norm_mix_g[0], w_in[0], conv_w[0], pool_w[0],
                 pool_scale[0], w_out[0], norm_mlp_g[0], w_ff1[0], w_ff2[0], norm_final_g)
    return out.reshape(batch, seq, d_model)
```

```python
import functools

import jax
import jax.numpy as jnp
from jax.experimental import pallas as pl
from jax.experimental.pallas import tpu as pltpu

CONV_K = 3
POOL_WINDOWS = (2, 4, 8, 16)
EPS = 1e-6

V7X_SUBLANES = 8
V7X_LANES = 128
V7X_MXU_DIM = 256
V7X_VMEM_BYTES = 64 * 1024 * 1024

ROW_TILE = 512
FF_CHUNK = 1024
CONV_HALO = V7X_SUBLANES
POOL_HALO = 2 * V7X_SUBLANES


def _rmsnorm(x, g):
    r = jax.lax.rsqrt(jnp.mean(x * x, axis=-1, keepdims=True) + EPS)
    return (x * r) * g


def _delay_rows(a, k):
    return pltpu.roll(a, k, axis=0)


def _bdot(a, b):
    return jnp.dot(a, b, preferred_element_type=jnp.float32)


def _block_kernel(x_ref, g_mix_ref, w_in_ref, conv_w_ref, pool_w_ref, pool_scale_ref,
                  w_out_ref, g_mlp_ref, w_ff1_ref, w_ff2_ref, g_final_ref,
                  o_ref, z_hist, v_hist, *, conv_width, pool_group_width):
    tile = x_ref.shape[0]
    d_ff = w_ff1_ref.shape[1]
    step = pl.program_id(0)
    bf16 = jnp.bfloat16

    @pl.when(step == 0)
    def _():
        z_hist[...] = jnp.zeros_like(z_hist)
        v_hist[...] = jnp.zeros_like(v_hist)

    x = x_ref[...]
    xn = _rmsnorm(x, g_mix_ref[...])
    proj = _bdot(xn.astype(bf16), w_in_ref[...])
    c = conv_width
    h, gate_b, gate_c, v = proj[:, 0:c], proj[:, c:2 * c], proj[:, 2 * c:3 * c], proj[:, 3 * c:]

    z = gate_c * h
    z_ext = jnp.concatenate([z_hist[...], z], axis=0)
    conv_w = conv_w_ref[...]
    conv = conv_w[CONV_K - 1:CONV_K, :] * z
    for k in range(CONV_K - 1):
        conv = conv + conv_w[k:k + 1, :] * _delay_rows(z_ext, CONV_K - 1 - k)[CONV_HALO:, :]
    y_conv = gate_b * conv
    z_hist[...] = z[tile - CONV_HALO:, :]

    v_ext = jnp.concatenate([v_hist[...], v], axis=0)
    pos = step * tile + jax.lax.broadcasted_iota(jnp.int32, (tile, 1), 0)
    frames_seen = (pos + 1).astype(jnp.float32)
    gw = pool_group_width
    us = []
    for gi, w in enumerate(POOL_WINDOWS):
        lo = gi * gw
        s = v_ext[:, lo:lo + gw]
        span = 1
        while span < w:
            s = s + _delay_rows(s, span)
            span *= 2
        us.append(s[POOL_HALO:, :] / jnp.minimum(frames_seen, float(w)) - v[:, lo:lo + gw])
    v_hist[...] = v[tile - POOL_HALO:, :]
    u = jnp.concatenate(us, axis=-1).astype(bf16)
    mx = V7X_MXU_DIM
    y_pool = jnp.concatenate(
        [_bdot(u[:, j * mx:(j + 1) * mx], pool_w_ref[j]) for j in range(pool_w_ref.shape[0])],
        axis=-1) * pool_scale_ref[...]

    y = jnp.concatenate([y_conv, y_pool], axis=-1).astype(bf16)
    x = x + _bdot(y, w_out_ref[...])

    xn = _rmsnorm(x, g_mlp_ref[...]).astype(bf16)
    for f0 in range(0, d_ff, FF_CHUNK):
        hid = jnp.square(jnp.maximum(_bdot(xn, w_ff1_ref[:, f0:f0 + FF_CHUNK]), 0.0))
        x = x + _bdot(hid.astype(bf16), w_ff2_ref[f0:f0 + FF_CHUNK, :])

    o_ref[...] = _rmsnorm(x, g_final_ref[...])


def _block_diag_tiles(pool_w, tile_dim):
    g, cg, _ = pool_w.shape
    per = tile_dim // cg
    tiles = []
    for j in range(g // per):
        t = jnp.zeros((tile_dim, tile_dim), pool_w.dtype)
        for k in range(per):
            t = t.at[k * cg:(k + 1) * cg, k * cg:(k + 1) * cg].set(pool_w[j * per + k])
        tiles.append(t)
    return jnp.stack(tiles)


def _resident(shape):
    return pl.BlockSpec(shape, lambda i: (0,) * len(shape), pipeline_mode=pl.Buffered(1))


def _layer(x, g_mix, w_in, conv_w, pool_w, pool_scale, w_out, g_mlp, w_ff1, w_ff2, g_final):
    seq, d_model = x.shape
    conv_width = conv_w.shape[1]
    pool_groups, pool_group_width, _ = pool_w.shape
    pool_width = pool_groups * pool_group_width
    assert seq % ROW_TILE == 0 and w_ff1.shape[1] % FF_CHUNK == 0
    assert pool_groups == len(POOL_WINDOWS) and max(POOL_WINDOWS) - 1 <= POOL_HALO
    assert all(w & (w - 1) == 0 for w in POOL_WINDOWS), "the doubling sum needs power-of-two windows"
    assert conv_w.shape[0] == CONV_K and CONV_K - 1 <= CONV_HALO
    assert w_in.shape[1] == 3 * conv_width + pool_width
    assert V7X_MXU_DIM % pool_group_width == 0 and pool_width % V7X_MXU_DIM == 0

    bf16 = jnp.bfloat16
    operands = (
        x,
        g_mix.reshape(1, d_model),
        w_in.astype(bf16),
        conv_w,
        _block_diag_tiles(pool_w.astype(bf16), V7X_MXU_DIM),
        pool_scale.reshape(1, pool_width),
        w_out.astype(bf16),
        g_mlp.reshape(1, d_model),
        w_ff1.astype(bf16),
        w_ff2.astype(bf16),
        g_final.reshape(1, d_model),
    )
    row_spec = pl.BlockSpec((ROW_TILE, d_model), lambda i: (i, 0))
    in_specs = [row_spec] + [_resident(a.shape) for a in operands[1:]]
    body = functools.partial(_block_kernel, conv_width=conv_width,
                             pool_group_width=pool_group_width)
    return pl.pallas_call(
        body,
        grid=(seq // ROW_TILE,),
        in_specs=in_specs,
        out_specs=row_spec,
        out_shape=jax.ShapeDtypeStruct((seq, d_model), x.dtype),
        scratch_shapes=[
            pltpu.VMEM((CONV_HALO, conv_width), jnp.float32),
            pltpu.VMEM((POOL_HALO, pool_width), jnp.float32),
        ],
        compiler_params=pltpu.CompilerParams(
            dimension_semantics=("arbitrary",),
            vmem_limit_bytes=V7X_VMEM_BYTES * 7 // 8,
        ),
        name="hybrid_block",
    )(*operands)


def kernel(x, norm_mix_g, w_in, conv_w, pool_w, pool_scale, w_out, norm_mlp_g, w_ff1, w_ff2, norm_final_g):
    batch, seq, d_model = x.shape
    depth = w_in.shape[0]
    assert depth == 1, "the final norm is fused into the (single) layer call"
    assert batch == 1
    out = _layer(x.reshape(seq, d_model), norm_mix_g[0], w_in[0], conv_w[0], pool_w[0],
                 pool_scale[0], w_out[0], norm_mlp_g[0], w_ff1[0], w_ff2[0], norm_final_g)
    return out.reshape(batch, seq, d_model)
```
